```python
import math
import jax, jax.numpy as jnp
from jax import lax
import numpy as np

D_MODEL = 1024
BATCH = 4
SEQ = 4096
DEPTH = 2
DEC_BATCH = 32
DEC_SEQ = 8
PAST_LEN = 16384
PAGE_SIZE = 128

N_META = 16
N_A_LAYERS = DEPTH // 2
N_B_LAYERS = DEPTH - N_A_LAYERS
A_HEADS = 8
A_FDIM = 128
A_IDIM = D_MODEL // A_HEADS
A_FWIDTH = A_HEADS * A_FDIM
A_WIDTH = A_HEADS * A_IDIM
A_CHUNK = 64
B_HEADS = 8
B_DH = D_MODEL // B_HEADS // 2
B_WIDTH = B_HEADS * 2 * B_DH
Q_BLOCK = 128
DEEPNORM_ALPHA = (2.0 * DEPTH) ** 0.25
DEEPNORM_BETA = (8.0 * DEPTH) ** -0.25
LN_EPS = 1e-5
RMS_EPS = 1e-6

kernel_name = 'yoco_hgrn2_diffattn_alibi_step'

F32 = jnp.float32


def layer_norm(x, g, b):
    xf = x.astype(F32)
    mu = xf.mean(-1, keepdims=True)
    var = jnp.square(xf - mu).mean(-1, keepdims=True)
    return ((xf - mu) * lax.rsqrt(var + LN_EPS) * g.astype(F32) + b.astype(F32)).astype(x.dtype)


def rms_norm(x, g):
    xf = x.astype(F32)
    return xf * lax.rsqrt(jnp.mean(xf * xf, -1, keepdims=True) + RMS_EPS) * g.astype(F32)


def alibi_slopes():
    return 2.0 ** (-8.0 * (jnp.arange(B_HEADS, dtype=F32) + 1.0) / B_HEADS)


def hgrn_chunk(q, k, v, log_f, s0):
    b = jnp.cumsum(log_f, axis=1)
    c = q.shape[1]
    causal = jnp.tril(jnp.ones((c, c), bool))
    decay = jnp.exp(jnp.where(causal[None, :, :, None, None],
                              b[:, :, None] - b[:, None, :], -jnp.inf))
    scores = jnp.einsum('nthf,nshf,ntshf->nhts', q, k, decay)
    o = (jnp.einsum('nthf,nhfi->nthi', q * jnp.exp(b), s0)
         + jnp.einsum('nhts,nshi->nthi', scores, v))
    b_last = b[:, -1]
    s_new = (jnp.exp(b_last)[..., None] * s0
             + jnp.einsum('nshf,nshi->nhfi', k * jnp.exp(b_last[:, None] - b), v))
    return o, s_new


def hgrn_prompt_scan(q, k, v, log_f):
    bsz = q.shape[0]
    s0 = jnp.zeros((bsz, A_HEADS, A_FDIM, A_IDIM), F32)
    o_meta, s = hgrn_chunk(q[:, :N_META], k[:, :N_META], v[:, :N_META], log_f[:, :N_META], s0)
    n_chunks = (q.shape[1] - N_META) // A_CHUNK

    def to_chunks(t):
        t = t[:, N_META:]
        return jnp.moveaxis(t.reshape(bsz, n_chunks, A_CHUNK, *t.shape[2:]), 1, 0)

    def step(state, inp):
        qc, kc, vc, fc = inp
        o, state = hgrn_chunk(qc, kc, vc, fc, state)
        return state, o

    s, o_real = lax.scan(step, s, (to_chunks(q), to_chunks(k), to_chunks(v), to_chunks(log_f)))
    o_real = jnp.moveaxis(o_real, 0, 1).reshape(bsz, n_chunks * A_CHUNK, A_HEADS, A_IDIM)
    return jnp.concatenate([o_meta, o_real], axis=1), s


def hgrn_mixer(h, w_in, lb, gn_g, w_out, s0):
    n, t = h.shape[:2]
    proj = h @ w_in
    q, f_pre, inp, gate = jnp.split(proj, [A_FWIDTH, 2 * A_FWIDTH, 2 * A_FWIDTH + A_WIDTH], axis=-1)
    q = q.astype(F32).reshape(n, t, A_HEADS, A_FDIM)
    lb = lb.reshape(A_HEADS, A_FDIM)
    f = lb + (1.0 - lb) * jax.nn.sigmoid(f_pre.astype(F32).reshape(n, t, A_HEADS, A_FDIM))
    log_f = jnp.log(f)
    k = 1.0 - f
    v = inp.astype(F32).reshape(n, t, A_HEADS, A_IDIM)
    if s0 is None:
        o, s_new = hgrn_prompt_scan(q, k, v, log_f)
    else:
        o, s_new = hgrn_chunk(q, k, v, log_f, s0.astype(F32))
    o = rms_norm(o, gn_g).reshape(n, t, A_WIDTH).astype(h.dtype)
    out = (o * jax.nn.silu(gate)) @ w_out
    return out, s_new.astype(h.dtype)


def diff_attention(q, k, v, qpos, kpos, lam):
    qf = q.astype(F32) * (B_DH ** -0.5)
    kf = k.astype(F32)
    s1 = jnp.einsum('...qhd,...khd->...hqk', qf[..., :B_DH], kf[..., :B_DH])
    s2 = jnp.einsum('...qhd,...khd->...hqk', qf[..., B_DH:], kf[..., B_DH:])
    dist = (qpos[:, None] - kpos[None, :]).astype(F32)
    bias = jnp.where(dist[None] >= 0, -alibi_slopes()[:, None, None] * dist[None], -jnp.inf)
    a = jax.nn.softmax(s1 + bias, axis=-1) - lam * jax.nn.softmax(s2 + bias, axis=-1)
    return jnp.einsum('...hqk,...khe->...qhe', a, v.astype(F32))


def diff_output(o, gate, lam_init, subln_g, w_out, dtype):
    o = rms_norm(o, subln_g) * (1.0 - lam_init)
    o = o.reshape(*o.shape[:-2], B_WIDTH).astype(dtype)
    return (o * jax.nn.silu(gate)) @ w_out


def diff_lambda(lam_vecs, lam_init):
    lv = lam_vecs.astype(F32)
    return jnp.exp(jnp.sum(lv[0] * lv[1])) - jnp.exp(jnp.sum(lv[2] * lv[3])) + lam_init


def diff_layer_prompt(h, k, v, w_in, lam, lam_init, subln_g, w_out):
    bsz, t = h.shape[:2]
    q, gate = jnp.split(h @ w_in, 2, axis=-1)
    n_blk = t // Q_BLOCK
    qb = jnp.swapaxes(q.reshape(bsz, n_blk, Q_BLOCK, B_HEADS, 2 * B_DH), 0, 1)
    kpos = jnp.arange(k.shape[1], dtype=jnp.int32)

    def block(args):
        q_blk, i = args
        qpos = N_META + i * Q_BLOCK + jnp.arange(Q_BLOCK, dtype=jnp.int32)
        return diff_attention(q_blk, k, v, qpos, kpos, lam)

    o = lax.map(block, (qb, jnp.arange(n_blk, dtype=jnp.int32)))
    o = jnp.swapaxes(o, 0, 1).reshape(bsz, t, B_HEADS, 2 * B_DH)
    return diff_output(o, gate, lam_init, subln_g, w_out, h.dtype)


def diff_layer_sample(h, k_new, v_new, cache_k, cache_v, page_table, w_in, lam, lam_init, subln_g, w_out):
    n, t = h.shape[:2]
    q, gate = jnp.split(h @ w_in, 2, axis=-1)
    q = q.reshape(n, t, B_HEADS, 2 * B_DH)
    past = page_table.shape[1] * PAGE_SIZE
    kpos = jnp.arange(past + t, dtype=jnp.int32)
    qpos = past + jnp.arange(t, dtype=jnp.int32)

    def one(args):
        pt, qs, kn, vn = args
        kp = cache_k[pt].reshape(past, B_HEADS, 2 * B_DH)
        vp = cache_v[pt].reshape(past, B_HEADS, 2 * B_DH)
        k_all = jnp.concatenate([kp, kn.astype(kp.dtype)], axis=0)
        v_all = jnp.concatenate([vp, vn.astype(vp.dtype)], axis=0)
        return diff_attention(qs, k_all, v_all, qpos, kpos, lam)

    o = lax.map(one, (page_table, q, k_new, v_new))
    return diff_output(o, gate, lam_init, subln_g, w_out, h.dtype)


def setup_inputs(seed: int = 0) -> dict:
    key = jax.random.key(seed)
    ks = jax.random.split(key, 24)
    n_pages = PAST_LEN // PAGE_SIZE
    used = DEC_BATCH * n_pages
    n_pool = used + max(1, used // 4)

    def nrm(k, shape, s):
        return jax.random.normal(k, shape, F32) * s

    page_table = jax.random.permutation(ks[5], n_pool)[:used].reshape(DEC_BATCH, n_pages).astype(jnp.int32)
    return {
        'x_prompt': nrm(ks[0], (BATCH, SEQ, D_MODEL), 1.0),
        'x_sample': nrm(ks[1], (DEC_BATCH, DEC_SEQ, D_MODEL), 1.0),
        'state_hgrn': nrm(ks[2], (N_A_LAYERS, DEC_BATCH, A_HEADS, A_FDIM, A_IDIM), 0.5),
        'cache_k': nrm(ks[3], (n_pool, PAGE_SIZE, B_HEADS, 2 * B_DH), 1.0),
        'cache_v': nrm(ks[4], (n_pool, PAGE_SIZE, B_HEADS, 2 * B_DH), 1.0),
        'page_table': page_table,
        'meta_tokens': nrm(ks[6], (N_META, D_MODEL), 1.0),
        'w_in_a': nrm(ks[7], (N_A_LAYERS, D_MODEL, 2 * A_FWIDTH + 2 * A_WIDTH), D_MODEL ** -0.5),
        'lb_logits_a': nrm(ks[8], (N_A_LAYERS + 1, A_FWIDTH), 0.1),
        'gn_a': 1.0 + nrm(ks[9], (N_A_LAYERS, A_IDIM), 0.02),
        'w_out_a': nrm(ks[10], (N_A_LAYERS, A_WIDTH, D_MODEL), DEEPNORM_BETA * A_WIDTH ** -0.5),
        'ln_g_a': 1.0 + nrm(ks[11], (N_A_LAYERS, D_MODEL), 0.02),
        'ln_b_a': nrm(ks[12], (N_A_LAYERS, D_MODEL), 0.02),
        'w_kv': nrm(ks[13], (D_MODEL, 2 * B_WIDTH), D_MODEL ** -0.5),
        'w_in_b': nrm(ks[14], (N_B_LAYERS, D_MODEL, 2 * B_WIDTH), D_MODEL ** -0.5),
        'lambda_b': nrm(ks[15], (N_B_LAYERS, 4, B_DH), 0.1),
        'subln_b': 1.0 + nrm(ks[16], (N_B_LAYERS, 2 * B_DH), 0.02),
        'w_out_b': nrm(ks[17], (N_B_LAYERS, B_WIDTH, D_MODEL), DEEPNORM_BETA * B_WIDTH ** -0.5),
        'ln_g_b': 1.0 + nrm(ks[18], (N_B_LAYERS, D_MODEL), 0.02),
        'ln_b_b': nrm(ks[19], (N_B_LAYERS, D_MODEL), 0.02),
    }


def reference(x_prompt, x_sample, state_hgrn, cache_k, cache_v, page_table, meta_tokens,
              w_in_a, lb_logits_a, gn_a, w_out_a, ln_g_a, ln_b_a, w_kv,
              w_in_b, lambda_b, subln_b, w_out_b, ln_g_b, ln_b_b):
    bsz = x_prompt.shape[0]
    meta = jnp.broadcast_to(meta_tokens[None].astype(x_prompt.dtype), (bsz, N_META, D_MODEL))
    hp = jnp.concatenate([meta, x_prompt], axis=1)
    hs = x_sample
    lb_all = jnp.cumsum(jax.nn.softmax(lb_logits_a.astype(F32), axis=0), axis=0)
    st_p, st_s = [], []
    k_p = v_p = k_s = v_s = None
    for i in range(DEPTH):
        if i < N_A_LAYERS:
            out_p, sp = hgrn_mixer(hp, w_in_a[i], lb_all[i], gn_a[i], w_out_a[i], None)
            out_s, ss = hgrn_mixer(hs, w_in_a[i], lb_all[i], gn_a[i], w_out_a[i], state_hgrn[i])
            hp = layer_norm(DEEPNORM_ALPHA * hp + out_p, ln_g_a[i], ln_b_a[i])
            hs = layer_norm(DEEPNORM_ALPHA * hs + out_s, ln_g_a[i], ln_b_a[i])
            st_p.append(sp)
            st_s.append(ss)
            if i == N_A_LAYERS - 1:
                k_p, v_p = jnp.split((hp @ w_kv).reshape(bsz, hp.shape[1], 2, B_HEADS, 2 * B_DH), 2, axis=2)
                k_p, v_p = k_p[:, :, 0], v_p[:, :, 0]
                k_s, v_s = jnp.split((hs @ w_kv).reshape(hs.shape[0], hs.shape[1], 2, B_HEADS, 2 * B_DH), 2, axis=2)
                k_s, v_s = k_s[:, :, 0], v_s[:, :, 0]
                hp = hp[:, N_META:]
        else:
            j = i - N_A_LAYERS
            lam_init = 0.8 - 0.6 * math.exp(-0.3 * i)
            lam = diff_lambda(lambda_b[j], lam_init)
            out_p = diff_layer_prompt(hp, k_p, v_p, w_in_b[j], lam, lam_init, subln_b[j], w_out_b[j])
            out_s = diff_layer_sample(hs, k_s, v_s, cache_k, cache_v, page_table, w_in_b[j], lam,
                                      lam_init, subln_b[j], w_out_b[j])
            hp = layer_norm(DEEPNORM_ALPHA * hp + out_p, ln_g_b[j], ln_b_b[j])
            hs = layer_norm(DEEPNORM_ALPHA * hs + out_s, ln_g_b[j], ln_b_b[j])
    y_prompt = hp
    y_sample = hs
    state_hgrn_prompt = jnp.stack(st_p)
    state_hgrn_sample = jnp.stack(st_s)
    return (y_prompt, y_sample, state_hgrn_prompt, state_hgrn_sample, k_p, v_p, k_s, v_s)
```

```python
import functools
import math

import jax
import jax.numpy as jnp
import numpy as np
from jax import lax
from jax.experimental import pallas as pl
from jax.experimental.pallas import tpu as pltpu

F32 = jnp.float32
BF16 = jnp.bfloat16

D_MODEL = 1024
N_HEADS = 8
HEAD_W = 128
HALF_W = 64
N_META = 16
PAGE = 128
DEPTH = 2
ALPHA = (2.0 * DEPTH) ** 0.25
LN_EPS = 1e-5
RMS_EPS = 1e-6
LAM_INIT = 0.8 - 0.6 * math.exp(-0.3 * 1)
NEG_BIG = -1e30
MASK_BIG = 2.0 ** 17

VMEM_LIMIT_BYTES = 56 * 1024 * 1024


def _dot(a, b):
    return jnp.dot(a, b, preferred_element_type=F32)


def _dot_nt(a, b):
    return lax.dot_general(a, b, (((1,), (1,)), ((), ())), preferred_element_type=F32)


def _dot_tn(a, b):
    return lax.dot_general(a, b, (((0,), (0,)), ((), ())), preferred_element_type=F32)


def _layer_norm(z, g, b):
    mu = jnp.mean(z, axis=-1, keepdims=True)
    zc = z - mu
    var = jnp.mean(zc * zc, axis=-1, keepdims=True)
    return zc * lax.rsqrt(var + LN_EPS) * g + b


def _silu(x):
    return x * jax.nn.sigmoid(x)


def _layer0_kernel(x_ref, s0_ref, w_in_ref, lbl_ref, gn_ref, w_out_ref, lng_ref, lnb_ref, w_post_ref,
                   *rest, seqs, chunks, chunk, n_post, attn_copies, mm_dtype):
    n_out = 1 + n_post + (2 if attn_copies else 0) + 1
    outs, scratch = rest[:n_out], rest[n_out:]
    h1_ref = outs[0]
    post_refs = outs[1:1 + n_post]
    copy_refs = outs[1 + n_post:n_out - 1]
    s_out_ref = outs[n_out - 1]
    q_s, b_s, k_s, v_s, g_s, o_s, y_s, st_s = scratch

    rows = seqs * chunks * chunk
    j = pl.program_id(1)

    def cast(v):
        return v.astype(mm_dtype)

    @pl.when(j == 0)
    def _():
        for g in range(seqs):
            for h in range(N_HEADS):
                st_s[g * N_HEADS + h] = s0_ref[g, h].T

    xb = x_ref[...].astype(BF16)
    q_s[...] = _dot(xb, w_in_ref[:, 0:D_MODEL])
    f_pre = _dot(xb, w_in_ref[:, D_MODEL:2 * D_MODEL])
    lbl = lbl_ref[...]
    e = jnp.exp(lbl - jnp.max(lbl, axis=0, keepdims=True))
    lb = e[0:1] / jnp.sum(e, axis=0, keepdims=True)
    f = lb + (1.0 - lb) * jax.nn.sigmoid(f_pre)
    b_s[...] = jnp.log(f)
    k_s[...] = 1.0 - f
    v_s[...] = _dot(xb, w_in_ref[:, 2 * D_MODEL:3 * D_MODEL])
    g_s[...] = _dot(xb, w_in_ref[:, 3 * D_MODEL:4 * D_MODEL])

    grp = min(rows, 64)
    shift = int(math.log2(chunk))
    ri = lax.broadcasted_iota(jnp.int32, (grp, grp), 0)
    ci = lax.broadcasted_iota(jnp.int32, (grp, grp), 1)
    tri = cast(jnp.where(((ri >> shift) == (ci >> shift)) & (ci <= ri), 1.0, 0.0))

    def cumsum_group(gi, carry):
        sl = pl.ds(pl.multiple_of(gi * grp, grp), grp)
        lf = b_s[sl, :]
        hi = lf.astype(BF16).astype(F32)
        r1 = lf - hi
        mid = r1.astype(BF16).astype(F32)
        lo = (r1 - mid).astype(BF16).astype(F32)
        b_s[sl, :] = _dot(tri, cast(hi)) + _dot(tri, cast(mid)) + _dot(tri, cast(lo))
        return carry

    lax.fori_loop(0, rows // grp, cumsum_group, 0)

    sub = min(16, chunk)
    n_sub = chunk // sub
    for g in range(seqs):
        def chunk_step(c, carry, g=g):
            r0 = pl.multiple_of((g * chunks + c) * chunk, chunk)
            sl = pl.ds(r0, chunk)
            for h in range(N_HEADS):
                ls = slice(h * HEAD_W, (h + 1) * HEAD_W)
                bh, qh, kh, vh = b_s[sl, ls], q_s[sl, ls], k_s[sl, ls], v_s[sl, ls]
                b_last = bh[chunk - 1:chunk, :]
                st = st_s[g * N_HEADS + h]
                vb = cast(vh)
                o_inter = _dot_nt(cast(qh * jnp.exp(bh)), cast(st))
                k_hat = cast(kh * jnp.exp(b_last - bh))
                st_s[g * N_HEADS + h] = st * jnp.exp(b_last) + _dot_tn(vb, k_hat)
                for i in range(n_sub):
                    lo_r, hi_r = sub * i, sub * (i + 1)
                    mid_r = lo_r + sub // 2
                    a = bh[mid_r - 1:mid_r, :]
                    q_i = cast(qh[lo_r:hi_r] * jnp.exp(bh[lo_r:hi_r] - a))
                    k_i = cast(kh[:hi_r] * jnp.exp(a - bh[:hi_r]))
                    att = _dot_nt(q_i, k_i)
                    rr = lax.broadcasted_iota(jnp.int32, (sub, hi_r), 0)
                    cc = lax.broadcasted_iota(jnp.int32, (sub, hi_r), 1)
                    att = jnp.where(cc <= rr + lo_r, att, 0.0)
                    o_i = o_inter[lo_r:hi_r] + _dot(cast(att), vb[:hi_r])
                    o_s[pl.ds(r0 + lo_r, sub), ls] = o_i
            return carry

        lax.fori_loop(0, chunks, chunk_step, 0)

    gn = gn_ref[...]
    for h in range(N_HEADS):
        ls = slice(h * HEAD_W, (h + 1) * HEAD_W)
        oh = o_s[:, ls]
        ms = jnp.mean(oh * oh, axis=-1, keepdims=True)
        y_s[:, ls] = (oh * lax.rsqrt(ms + RMS_EPS) * gn * _silu(g_s[:, ls])).astype(BF16)
    out = _dot(y_s[...], w_out_ref[...])
    h1 = _layer_norm(ALPHA * x_ref[...] + out, lng_ref[...], lnb_ref[...])
    h1_ref[...] = h1
    hb = h1.astype(BF16)
    for p in range(n_post):
        res = _dot(hb, w_post_ref[:, p * D_MODEL:(p + 1) * D_MODEL])
        post_refs[p][...] = res
        if attn_copies and p < 2:
            for h in range(N_HEADS):
                copy_refs[p][0, h] = res[:, h * HEAD_W:(h + 1) * HEAD_W].astype(BF16)

    @pl.when(j == pl.num_programs(1) - 1)
    def _():
        for g in range(seqs):
            for h in range(N_HEADS):
                s_out_ref[g, h] = st_s[g * N_HEADS + h].T


def _const_spec(shape):
    nd = len(shape)
    return pl.BlockSpec(shape, lambda i, j, _nd=nd: (0,) * _nd)


def _layer0(x, s0, w_in, lb_logits, gn, w_out, ln_g, ln_b, w_post, *, n_seq, seq_len, seqs, chunks,
            chunk, attn_copies, mm_dtype, s0_shared):
    rows = seqs * chunks * chunk
    nj = seq_len // (chunks * chunk)
    ni = n_seq // seqs
    assert n_seq % seqs == 0 and seq_len % (chunks * chunk) == 0
    assert seqs == 1 or nj == 1
    n_post = w_post.shape[1] // D_MODEL
    n_tok = n_seq * seq_len

    row_spec = pl.BlockSpec((rows, D_MODEL), lambda i, j: (i * nj + j, 0))
    state_block = (seqs, N_HEADS, HEAD_W, HEAD_W)
    s0_spec = pl.BlockSpec(state_block, (lambda i, j: (0, 0, 0, 0)) if s0_shared else (lambda i, j: (i, 0, 0, 0)))
    in_specs = [row_spec, s0_spec, _const_spec(w_in.shape), _const_spec(lb_logits.shape), _const_spec(gn.shape),
                _const_spec(w_out.shape), _const_spec(ln_g.shape), _const_spec(ln_b.shape),
                _const_spec(w_post.shape)]
    out_shape = [jax.ShapeDtypeStruct((n_tok, D_MODEL), F32)] * (1 + n_post)
    out_specs = [row_spec] * (1 + n_post)
    if attn_copies:
        out_shape += [jax.ShapeDtypeStruct((n_seq, N_HEADS, seq_len, HEAD_W), BF16)] * 2
        out_specs += [pl.BlockSpec((1, N_HEADS, rows, HEAD_W), lambda i, j: (i, 0, j, 0))] * 2
    out_shape.append(jax.ShapeDtypeStruct((n_seq, N_HEADS, HEAD_W, HEAD_W), F32))
    out_specs.append(pl.BlockSpec(state_block, lambda i, j: (i, 0, 0, 0)))

    scratch = [pltpu.VMEM((rows, D_MODEL), F32)] * 6
    scratch.append(pltpu.VMEM((rows, D_MODEL), BF16))
    scratch.append(pltpu.VMEM((seqs * N_HEADS, HEAD_W, HEAD_W), F32))

    kern = functools.partial(_layer0_kernel, seqs=seqs, chunks=chunks, chunk=chunk, n_post=n_post,
                             attn_copies=attn_copies, mm_dtype=mm_dtype)
    return pl.pallas_call(
        kern,
        grid=(ni, nj),
        in_specs=in_specs,
        out_specs=out_specs,
        out_shape=out_shape,
        scratch_shapes=scratch,
        compiler_params=pltpu.CompilerParams(dimension_semantics=("arbitrary", "arbitrary"),
                                             vmem_limit_bytes=VMEM_LIMIT_BYTES),
        name=f"layer0_r{rows}",
    )(x, s0, w_in, lb_logits, gn, w_out, ln_g, ln_b, w_post)


def _diff_lambda(lam_ref):
    lv = lam_ref[...]
    s1 = jnp.sum(lv[0:1] * lv[1:2], axis=-1, keepdims=True)
    s2 = jnp.sum(lv[2:3] * lv[3:4], axis=-1, keepdims=True)
    return jnp.exp(s1) - jnp.exp(s2) + LAM_INIT


def _diff_epilogue(head_out, gate_of, resid, y_s, subln_ref, w_out_ref, lng_ref, lnb_ref):
    sub = subln_ref[...]
    for h in range(N_HEADS):
        ls = slice(h * HEAD_W, (h + 1) * HEAD_W)
        oh = head_out(h)
        ms = jnp.mean(oh * oh, axis=-1, keepdims=True)
        on = oh * lax.rsqrt(ms + RMS_EPS) * sub * (1.0 - LAM_INIT)
        y_s[:, ls] = (on * _silu(gate_of(ls))).astype(BF16)
    out = _dot(y_s[...], w_out_ref[...])
    return _layer_norm(ALPHA * resid + out, lng_ref[...], lnb_ref[...])


def _prompt_attn_kernel(slopes_ref, h1_ref, kb_ref, vb_ref, km_ref, vm_ref, w_in_ref, lam_ref, subln_ref,
                        w_out_ref, lng_ref, lnb_ref, y_ref,
                        qs_s, gate_s, acc_s, m_s, l_s, o_s, y_s, *, tq, tk):
    qi = pl.program_id(1)
    n_diag = tq // tk
    q_start = qi * tq

    hb = h1_ref[...].astype(BF16)
    q = _dot(hb, w_in_ref[:, 0:D_MODEL]) * (HALF_W ** -0.5)
    gate_s[...] = _dot(hb, w_in_ref[:, D_MODEL:2 * D_MODEL])
    first_half = lax.broadcasted_iota(jnp.int32, (tq, HEAD_W), 1) < HALF_W
    for h in range(N_HEADS):
        qh = q[:, h * HEAD_W:(h + 1) * HEAD_W]
        qs_s[h, 0:tq] = jnp.where(first_half, qh, 0.0).astype(BF16)
        qs_s[h, tq:2 * tq] = jnp.where(first_half, 0.0, qh).astype(BF16)
    lam = _diff_lambda(lam_ref)

    def head_step(h, carry):
        qq = qs_s[h]
        slope = slopes_ref[h]
        m_s[...] = jnp.full(m_s.shape, NEG_BIG, F32)
        l_s[...] = jnp.zeros(l_s.shape, F32)
        acc_s[...] = jnp.zeros(acc_s.shape, F32)

        def block(k, v, rel0, masked):
            n = k.shape[0]
            s = _dot_nt(qq, k)
            col = lax.broadcasted_iota(jnp.int32, (1, n), 1)
            s = s + slope * (rel0 + col).astype(F32)
            if masked:
                row = lax.broadcasted_iota(jnp.int32, (2 * tq, n), 0) & (tq - 1)
                s = jnp.where(rel0 + col <= row, s, NEG_BIG)
            m_old = m_s[...]
            m_new = jnp.maximum(m_old, jnp.max(s, axis=-1, keepdims=True))
            alpha = jnp.exp(m_old - m_new)
            p = jnp.exp(s - m_new[:, 0:1])
            l_s[...] = alpha * l_s[...] + jnp.sum(p, axis=-1, keepdims=True)
            acc_s[...] = alpha * acc_s[...] + _dot(p.astype(BF16), v)
            m_s[...] = m_new

        block(km_ref[h], vm_ref[h], -N_META - q_start, False)

        def kv_step(jb, c):
            k0 = pl.multiple_of(jb * tk, tk)
            block(kb_ref[0, h, pl.ds(k0, tk), :], vb_ref[0, h, pl.ds(k0, tk), :], k0 - q_start, False)
            return c

        lax.fori_loop(0, qi * n_diag, kv_step, 0)
        for d in range(n_diag):
            k0 = pl.multiple_of(q_start + d * tk, tk)
            block(kb_ref[0, h, pl.ds(k0, tk), :], vb_ref[0, h, pl.ds(k0, tk), :], d * tk, True)

        o = acc_s[...] / l_s[...]
        o_s[h] = o[0:tq] - lam * o[tq:2 * tq]
        return carry

    lax.fori_loop(0, N_HEADS, head_step, 0)

    y_ref[...] = _diff_epilogue(lambda h: o_s[h], lambda ls: gate_s[:, ls], h1_ref[...], y_s,
                                subln_ref, w_out_ref, lng_ref, lnb_ref)


def _prompt_attn(slopes, h1, kb, vb, km, vm, w_in, lam_vecs, subln, w_out, ln_g, ln_b, *, tq, tk):
    n_b, _, seq, _ = kb.shape
    nq = seq // tq
    row_spec = pl.BlockSpec((tq, D_MODEL), lambda b, i: (b * nq + i, 0))
    kv_spec = pl.BlockSpec((1, N_HEADS, seq, HEAD_W), lambda b, i: (b, 0, 0, 0))
    in_specs = [pl.BlockSpec(memory_space=pltpu.SMEM), row_spec, kv_spec, kv_spec,
                _const_spec(km.shape), _const_spec(vm.shape), _const_spec(w_in.shape), _const_spec(lam_vecs.shape),
                _const_spec(subln.shape), _const_spec(w_out.shape), _const_spec(ln_g.shape), _const_spec(ln_b.shape)]
    scratch = [pltpu.VMEM((N_HEADS, 2 * tq, HEAD_W), BF16),
               pltpu.VMEM((tq, D_MODEL), F32),
               pltpu.VMEM((2 * tq, HEAD_W), F32),
               pltpu.VMEM((2 * tq, HEAD_W), F32),
               pltpu.VMEM((2 * tq, HEAD_W), F32),
               pltpu.VMEM((N_HEADS, tq, HEAD_W), F32),
               pltpu.VMEM((tq, D_MODEL), BF16)]
    return pl.pallas_call(
        functools.partial(_prompt_attn_kernel, tq=tq, tk=tk),
        grid=(n_b, nq),
        in_specs=in_specs,
        out_specs=row_spec,
        out_shape=jax.ShapeDtypeStruct(h1.shape, F32),
        scratch_shapes=scratch,
        compiler_params=pltpu.CompilerParams(dimension_semantics=("arbitrary", "arbitrary"),
                                             vmem_limit_bytes=VMEM_LIMIT_BYTES),
        name="prompt_attn",
    )(slopes, h1, kb, vb, km, vm, w_in, lam_vecs, subln, w_out, ln_g, ln_b)


def _sample_aug_constants(dec_seq):
    rows = N_HEADS * 2 * dec_seq
    r = np.arange(rows)
    r_head, r_tok = r // (2 * dec_seq), r % dec_seq
    q_aug = np.zeros((rows, HEAD_W), np.float32)
    q_aug[:, 0] = 2.0 ** (-(r_head + 1.0))
    q_aug[r, 1 + r_head] = 1.0
    q_aug[r, 1 + N_HEADS + r_tok] = 1.0
    c = np.arange(PAGE * N_HEADS)
    c_tok, c_head = c // N_HEADS, c % N_HEADS
    k_aug = np.zeros((PAGE * N_HEADS, HEAD_W), np.float32)
    k_aug[:, 0] = c_tok
    k_aug[:, 1:1 + N_HEADS] = -MASK_BIG * (c_head[:, None] != np.arange(N_HEADS)[None, :])
    n_aug = k_aug[:2 * dec_seq * N_HEADS].copy()
    n_tok = c_tok[:2 * dec_seq * N_HEADS]
    n_aug[:, 1 + N_HEADS:1 + N_HEADS + dec_seq] = -MASK_BIG * (n_tok[:, None] > np.arange(dec_seq)[None, :])
    slope_rows = np.broadcast_to(q_aug[:, 0:1], (rows, HEAD_W))
    return (jnp.asarray(q_aug, BF16), jnp.asarray(k_aug, BF16), jnp.asarray(n_aug, BF16),
            jnp.asarray(slope_rows, F32))


def _sample_attn_kernel(pt_ref, q_ref, kn_ref, vn_ref, qaug_ref, kaug_ref, naug_ref, srow_ref, lam_ref, *rest,
                        pps, dec_seq, past):
    ck_refs, cv_refs = rest[:pps], rest[pps:2 * pps]
    o_ref = rest[2 * pps]
    qf_s, qm_s, bias_s, m_s, l_s, acc_s = rest[2 * pps + 1:]
    j = pl.program_id(1)
    page_rows = PAGE * N_HEADS
    new_rows = 2 * dec_seq * N_HEADS

    @pl.when(j == 0)
    def _():
        q = q_ref[0] * (HALF_W ** -0.5)
        first_half = lax.broadcasted_iota(jnp.int32, (dec_seq, HEAD_W), 1) < HALF_W
        for h in range(N_HEADS):
            qh = q[:, h * HEAD_W:(h + 1) * HEAD_W]
            qf_s[2 * dec_seq * h:2 * dec_seq * h + dec_seq] = jnp.where(first_half, qh, 0.0)
            qf_s[2 * dec_seq * h + dec_seq:2 * dec_seq * (h + 1)] = jnp.where(first_half, 0.0, qh)
        qm_s[...] = qf_s[...].astype(BF16)
        bias_s[...] = _dot_nt(qaug_ref[...], kaug_ref[...])
        m_s[...] = jnp.full(m_s.shape, NEG_BIG, F32)
        l_s[...] = jnp.zeros(l_s.shape, F32)
        acc_s[...] = jnp.zeros(acc_s.shape, F32)

    slope_rows = srow_ref[...]
    qm = qm_s[...]

    def attend(state, k, v, bias, base):
        m_old, l_old, acc_old = state
        s = _dot_nt(qm, k) + bias
        off = slope_rows * base
        m_new = jnp.maximum(m_old, jnp.max(s, axis=-1, keepdims=True) + off)
        alpha = jnp.exp(m_old - m_new)
        p = jnp.exp(s - (m_new - off)[:, 0:1])
        l_new = alpha * l_old + jnp.sum(p, axis=-1, keepdims=True)
        acc_new = alpha * acc_old + _dot(p.astype(BF16), v)
        return m_new, l_new, acc_new

    state = (m_s[...], l_s[...], acc_s[...])
    for i in range(pps):
        k = ck_refs[i][0].reshape(page_rows, HEAD_W).astype(BF16)
        v = cv_refs[i][0].reshape(page_rows, HEAD_W).astype(BF16)
        base = ((j * pps + i) * PAGE - past).astype(F32)
        state = attend(state, k, v, bias_s[...], base)
    m_s[...], l_s[...], acc_s[...] = state

    @pl.when(j == pl.num_programs(1) - 1)
    def _():
        n_data = dec_seq * N_HEADS
        pad = jnp.zeros((new_rows - n_data, HEAD_W), BF16)
        k = jnp.concatenate([kn_ref[0].reshape(n_data, HEAD_W).astype(BF16), pad], axis=0)
        v = jnp.concatenate([vn_ref[0].reshape(n_data, HEAD_W).astype(BF16), pad], axis=0)
        bias_new = _dot_nt(qaug_ref[...], naug_ref[...])
        _, l_fin, acc_fin = attend((m_s[...], l_s[...], acc_s[...]), k, v, bias_new, jnp.float32(0.0))

        lam = _diff_lambda(lam_ref)
        o = acc_fin / l_fin
        for h in range(N_HEADS):
            r0 = 2 * dec_seq * h
            o_ref[0, :, h * HEAD_W:(h + 1) * HEAD_W] = o[r0:r0 + dec_seq] - lam * o[r0 + dec_seq:r0 + 2 * dec_seq]


def _sample_attn(page_table, q, k_new, v_new, cache_k, cache_v, lam_vecs, *, pps):
    n_seq, dec_seq, _ = q.shape
    n_pages = page_table.shape[1]
    past = n_pages * PAGE
    assert n_pages % pps == 0 and 2 * dec_seq * N_HEADS == HEAD_W
    q_aug, k_aug, n_aug, slope_rows = _sample_aug_constants(dec_seq)
    rows = N_HEADS * 2 * dec_seq

    def cst(shape):
        nd = len(shape)
        return pl.BlockSpec(shape, lambda b, j, pt, _nd=nd: (0,) * _nd)

    def page_spec(i):
        return pl.BlockSpec((1, PAGE, N_HEADS, HEAD_W), lambda b, j, pt, _i=i: (pt[b, j * pps + _i], 0, 0, 0))

    in_specs = [pl.BlockSpec((1, dec_seq, D_MODEL), lambda b, j, pt: (b, 0, 0)),
                pl.BlockSpec((1, dec_seq, N_HEADS, HEAD_W), lambda b, j, pt: (b, 0, 0, 0)),
                pl.BlockSpec((1, dec_seq, N_HEADS, HEAD_W), lambda b, j, pt: (b, 0, 0, 0)),
                cst(q_aug.shape), cst(k_aug.shape), cst(n_aug.shape), cst(slope_rows.shape), cst(lam_vecs.shape)]
    in_specs += [page_spec(i) for i in range(pps)] * 2
    scratch = [pltpu.VMEM((rows, HEAD_W), F32),
               pltpu.VMEM((rows, HEAD_W), BF16),
               pltpu.VMEM((rows, PAGE * N_HEADS), F32),
               pltpu.VMEM((rows, HEAD_W), F32),
               pltpu.VMEM((rows, HEAD_W), F32),
               pltpu.VMEM((rows, HEAD_W), F32)]
    grid_spec = pltpu.PrefetchScalarGridSpec(
        num_scalar_prefetch=1,
        grid=(n_seq, n_pages // pps),
        in_specs=in_specs,
        out_specs=pl.BlockSpec((1, dec_seq, D_MODEL), lambda b, j, pt: (b, 0, 0)),
        scratch_shapes=scratch,
    )
    return pl.pallas_call(
        functools.partial(_sample_attn_kernel, pps=pps, dec_seq=dec_seq, past=past),
        grid_spec=grid_spec,
        out_shape=jax.ShapeDtypeStruct((n_seq, dec_seq, D_MODEL), F32),
        compiler_params=pltpu.CompilerParams(dimension_semantics=("arbitrary", "arbitrary"),
                                             vmem_limit_bytes=VMEM_LIMIT_BYTES),
        name="sample_attn",
    )(page_table, q, k_new, v_new, q_aug, k_aug, n_aug, slope_rows, lam_vecs, *([cache_k] * pps), *([cache_v] * pps))


def _sample_out_kernel(o_ref, gate_ref, h1_ref, subln_ref, w_out_ref, lng_ref, lnb_ref, y_ref, y_s):
    y_ref[...] = _diff_epilogue(lambda h: o_ref[:, h * HEAD_W:(h + 1) * HEAD_W], lambda ls: gate_ref[:, ls],
                                h1_ref[...], y_s, subln_ref, w_out_ref, lng_ref, lnb_ref)


def _sample_out(o, gate, h1, subln, w_out, ln_g, ln_b):
    n = o.shape[0]
    return pl.pallas_call(
        _sample_out_kernel,
        out_shape=jax.ShapeDtypeStruct((n, D_MODEL), F32),
        scratch_shapes=[pltpu.VMEM((n, D_MODEL), BF16)],
        compiler_params=pltpu.CompilerParams(vmem_limit_bytes=VMEM_LIMIT_BYTES),
        name="sample_out",
    )(o, gate, h1, subln, w_out, ln_g, ln_b)


PROMPT_ROWS = 256
PROMPT_CHUNK = 64
ATTN_TQ = 256
ATTN_TK = 256
SAMPLE_SEQS_PER_BLOCK = 8
PAGES_PER_STEP = 8


def kernel(x_prompt, x_sample, state_hgrn, cache_k, cache_v, page_table, meta_tokens, w_in_a, lb_logits_a, gn_a,
           w_out_a, ln_g_a, ln_b_a, w_kv, w_in_b, lambda_b, subln_b, w_out_b, ln_g_b, ln_b_b):
    n_b, seq, _ = x_prompt.shape
    n_s, dec_seq, _ = x_sample.shape

    w_in_a0 = w_in_a[0].astype(BF16)
    w_out_a0 = w_out_a[0].astype(BF16)
    w_kv_b = w_kv.astype(BF16)
    w_in_b0 = w_in_b[0].astype(BF16)
    w_out_b0 = w_out_b[0].astype(BF16)
    gn = gn_a[0].reshape(1, HEAD_W)
    ln_g0, ln_b0 = ln_g_a[0].reshape(1, D_MODEL), ln_b_a[0].reshape(1, D_MODEL)
    ln_g1, ln_b1 = ln_g_b[0].reshape(1, D_MODEL), ln_b_b[0].reshape(1, D_MODEL)
    subln = subln_b[0].reshape(1, HEAD_W)
    lam_vecs = lambda_b[0]
    layer0 = functools.partial(_layer0, w_in=w_in_a0, lb_logits=lb_logits_a, gn=gn, w_out=w_out_a0,
                               ln_g=ln_g0, ln_b=ln_b0)

    zero_state = jnp.zeros((1, N_HEADS, HEAD_W, HEAD_W), F32)
    _, k_meta, v_meta, kb_meta, vb_meta, s_meta = layer0(
        meta_tokens, zero_state, w_post=w_kv_b, n_seq=1, seq_len=N_META, seqs=1, chunks=1, chunk=N_META,
        attn_copies=True, mm_dtype=F32, s0_shared=True)

    h1_p, k_real, v_real, kb, vb, s_prompt = layer0(
        x_prompt.reshape(n_b * seq, D_MODEL), s_meta, w_post=w_kv_b, n_seq=n_b, seq_len=seq, seqs=1,
        chunks=PROMPT_ROWS // PROMPT_CHUNK, chunk=PROMPT_CHUNK, attn_copies=True, mm_dtype=BF16, s0_shared=True)

    slopes = 2.0 ** (-8.0 * (jnp.arange(N_HEADS, dtype=F32) + 1.0) / N_HEADS)
    y_prompt = _prompt_attn(slopes, h1_p, kb, vb, kb_meta[0], vb_meta[0], w_in_b0, lam_vecs, subln, w_out_b0,
                            ln_g1, ln_b1, tq=ATTN_TQ, tk=ATTN_TK).reshape(n_b, seq, D_MODEL)

    def with_meta(meta_rows, real_rows):
        meta_b = jnp.broadcast_to(meta_rows[None], (n_b, N_META, D_MODEL))
        full = jnp.concatenate([meta_b, real_rows.reshape(n_b, seq, D_MODEL)], axis=1)
        return full.reshape(n_b, N_META + seq, N_HEADS, HEAD_W)

    k_prompt, v_prompt = with_meta(k_meta, k_real), with_meta(v_meta, v_real)

    w_post_s = jnp.concatenate([w_kv_b, w_in_b0], axis=1)
    h1_s, k_s, v_s, q_s, gate_s, s_sample = layer0(
        x_sample.reshape(n_s * dec_seq, D_MODEL), state_hgrn[0], w_post=w_post_s, n_seq=n_s, seq_len=dec_seq,
        seqs=SAMPLE_SEQS_PER_BLOCK, chunks=1, chunk=dec_seq, attn_copies=False, mm_dtype=F32, s0_shared=False)
    k_sample = k_s.reshape(n_s, dec_seq, N_HEADS, HEAD_W)
    v_sample = v_s.reshape(n_s, dec_seq, N_HEADS, HEAD_W)
    o_s = _sample_attn(page_table, q_s.reshape(n_s, dec_seq, D_MODEL), k_sample, v_sample, cache_k, cache_v,
                       lam_vecs, pps=PAGES_PER_STEP)
    y_sample = _sample_out(o_s.reshape(n_s * dec_seq, D_MODEL), gate_s, h1_s, subln, w_out_b0, ln_g1, ln_b1)

    return (y_prompt, y_sample.reshape(n_s, dec_seq, D_MODEL), s_prompt[None], s_sample[None],
            k_prompt, v_prompt, k_sample, v_sample)
```
